```python
import math
import jax, jax.numpy as jnp
from jax import lax
import numpy as np

D_MODEL = 2048
BATCH = 16
SEQ = 256
DEPTH = 2
DEC_BATCH = 4
DEC_SEQ = 4096
PAST_LEN = 256

GRID_W = 64
HEAD_DIM_A = 64
W_A = D_MODEL // 2
N_HEADS_A = W_A // (2 * HEAD_DIM_A)
W_B = D_MODEL // 2
W_C = D_MODEL // 2
POOL_WINDOWS = (2, 4, 8, 16)
POOL_GROUP = W_C // len(POOL_WINDOWS)
N_BRANCH = 3
IN_COLS = 3 * W_A + 3 * W_B + W_C
IN_SPLITS = (W_A, 2 * W_A, 3 * W_A, 3 * W_A + W_B, 3 * W_A + 2 * W_B, 3 * W_A + 3 * W_B)
N_EXPERTS = 16
CAPACITY_FACTOR = 2
D_FF_EXPERT = D_MODEL
ROPE_BASE = 10000.0
Q_BLOCK = 128
EPS = 1e-6

kernel_name = "hybrid_diffattn_conv_pool_ecmoe_ctxprefix_step"


def rmsnorm(x, g):
    x32 = x.astype(jnp.float32)
    y = x32 * lax.rsqrt(jnp.mean(x32 * x32, axis=-1, keepdims=True) + EPS)
    return (y * g.astype(jnp.float32)).astype(x.dtype)


def _rotate(x, ang):
    half = x.shape[-1] // 2
    x1, x2 = x[..., :half], x[..., half:]
    cos = jnp.cos(ang).astype(x.dtype)
    sin = jnp.sin(ang).astype(x.dtype)
    return jnp.concatenate([x1 * cos - x2 * sin, x2 * cos + x1 * sin], axis=-1)


def rope_2d(x):
    n = x.shape[1]
    rows = n // GRID_W
    row = jnp.repeat(jnp.arange(rows), GRID_W).astype(jnp.float32)
    col = jnp.tile(jnp.arange(GRID_W), rows).astype(jnp.float32)
    axis_dim = HEAD_DIM_A // 2
    n_freq = axis_dim // 2
    inv = 1.0 / (ROPE_BASE ** (jnp.arange(n_freq, dtype=jnp.float32) / n_freq))
    ang_r = (row[:, None] * inv)[None, :, None, :]
    ang_c = (col[:, None] * inv)[None, :, None, :]
    return jnp.concatenate([_rotate(x[..., :axis_dim], ang_r), _rotate(x[..., axis_dim:], ang_c)], axis=-1)


def rope_pair(x):
    return jnp.concatenate([rope_2d(x[..., :HEAD_DIM_A]), rope_2d(x[..., HEAD_DIM_A:])], axis=-1)


def diff_attention(q, k, v, lam):
    b, n, h, e = q.shape
    nb = n // Q_BLOCK
    k1, k2 = k[..., :HEAD_DIM_A], k[..., HEAD_DIM_A:]
    scale = HEAD_DIM_A ** -0.5

    def one_block(qb):
        q1, q2 = qb[..., :HEAD_DIM_A], qb[..., HEAD_DIM_A:]
        s1 = jnp.einsum('bqhd,bkhd->bhqk', q1, k1).astype(jnp.float32) * scale
        s2 = jnp.einsum('bqhd,bkhd->bhqk', q2, k2).astype(jnp.float32) * scale
        a = jax.nn.softmax(s1, axis=-1) - lam * jax.nn.softmax(s2, axis=-1)
        return jnp.einsum('bhqk,bkhe->bqhe', a.astype(v.dtype), v)

    qb = q.reshape(b, nb, Q_BLOCK, h, e).transpose(1, 0, 2, 3, 4)
    o = lax.map(one_block, qb)
    return o.transpose(1, 0, 2, 3, 4).reshape(b, n, h, e)


def short_conv3(x, w):
    xp = jnp.pad(x, ((0, 0), (1, 1), (0, 0)))
    return xp[:, :-2] * w[0] + xp[:, 1:-1] * w[1] + xp[:, 2:] * w[2]


def centred_mean(x, window):
    b, n, ch = x.shape
    cs = jnp.concatenate([jnp.zeros((b, 1, ch), jnp.float32), jnp.cumsum(x.astype(jnp.float32), axis=1)], axis=1)
    t = jnp.arange(n)
    lo = jnp.clip(t - window // 2, 0, n)
    hi = jnp.clip(t + window // 2, 0, n)
    cnt = (hi - lo).astype(jnp.float32)
    return ((cs[:, hi] - cs[:, lo]) / cnt[None, :, None]).astype(x.dtype)


def pool_mixer(x, pool_w, pool_scale):
    outs = []
    for g, win in enumerate(POOL_WINDOWS):
        xg = x[..., g * POOL_GROUP:(g + 1) * POOL_GROUP]
        outs.append(jnp.einsum('bnc,cd->bnd', centred_mean(xg, win) - xg, pool_w[g]))
    return jnp.concatenate(outs, axis=-1) * pool_scale


def token_mixer(h, lp, lam_init, ctx_kv):
    b, n, _ = h.shape
    q, k, v, cb, cc, ch, pin = jnp.split(h @ lp['w_in'], IN_SPLITS, axis=-1)
    q = q.reshape(b, n, N_HEADS_A, 2 * HEAD_DIM_A)
    k = k.reshape(b, n, N_HEADS_A, 2 * HEAD_DIM_A)
    v = v.reshape(b, n, N_HEADS_A, 2 * HEAD_DIM_A)
    if ctx_kv is None:
        q_use, k_all, v_all = q, k, v
    else:
        ctx_k, ctx_v = ctx_kv
        q_use = rope_pair(q)
        k_all = jnp.concatenate([rope_pair(k), ctx_k.astype(k.dtype)], axis=1)
        v_all = jnp.concatenate([v, ctx_v.astype(v.dtype)], axis=1)
    lp_ = lp['lam_p'].astype(jnp.float32)
    lam = jnp.exp(jnp.sum(lp_[0] * lp_[1])) - jnp.exp(jnp.sum(lp_[2] * lp_[3])) + lam_init
    o = diff_attention(q_use, k_all, v_all, lam)
    o = (rmsnorm(o, lp['g_subln']) * (1.0 - lam_init)).reshape(b, n, W_A)
    yb = cb * short_conv3(cc * ch, lp['conv_w'])
    yc = pool_mixer(pin, lp['pool_w'], lp['pool_scale'])
    wb = lp['w_branch']
    ga, gb, gc = jnp.split(jax.nn.sigmoid(h @ lp['w_gate'] + lp['b_gate']), N_BRANCH, axis=-1)
    merged = ga * (o @ wb[0]) + gb * (yb @ wb[1]) + gc * (yc @ wb[2])
    return merged @ lp['w_out'], k, v


def expert_choice_ffn(h, w_router, w1, w3, w2):
    b, n, d = h.shape
    cap = CAPACITY_FACTOR * n // N_EXPERTS
    aff = jax.nn.softmax((h @ w_router).astype(jnp.float32), axis=-1)
    gate, idx = lax.top_k(jnp.swapaxes(aff, 1, 2), cap)

    def per_request(hr, g, i):
        xs = hr[i]
        a = jnp.einsum('ecd,edf->ecf', xs, w1)
        u = jnp.einsum('ecd,edf->ecf', xs, w3)
        y = jnp.einsum('ecf,efd->ecd', jax.nn.silu(a) * u, w2) * g[..., None].astype(hr.dtype)
        return jnp.zeros_like(hr).at[i.reshape(-1)].add(y.reshape(-1, d))

    return jax.vmap(per_request)(h, gate, idx)


def modulate(x, g, shift, scale):
    return rmsnorm(x, g) * (1.0 + scale) + shift


def trunk_layer(x, cvec, lp, lam_init, ctx_kv):
    mod = jax.nn.silu(cvec) @ lp['w_mod'] + lp['b_mod']
    sh1, sc1, gt1, sh2, sc2, gt2 = [m[:, None, :] for m in jnp.split(mod, 6, axis=-1)]
    g = lp['g_norm']
    hmix, k, v = token_mixer(modulate(x, g[0], sh1, sc1), lp, lam_init, ctx_kv)
    x = x + gt1 * rmsnorm(hmix, g[1])
    hff = expert_choice_ffn(modulate(x, g[2], sh2, sc2), lp['w_router'], lp['w_e1'], lp['w_e3'], lp['w_e2'])
    x = x + gt2 * rmsnorm(hff, g[3])
    return x, k, v


def setup_inputs(seed: int = 0) -> dict:
    key = jax.random.key(seed)
    ks = jax.random.split(key, 24)
    nrm = jax.random.normal
    f32 = jnp.float32
    D = D_MODEL
    kv_shape = (DEC_BATCH, DEPTH, PAST_LEN, N_HEADS_A, 2 * HEAD_DIM_A)
    return {
        'x_prompt': nrm(ks[0], (BATCH, SEQ, D), f32),
        'x_sample': nrm(ks[1], (DEC_BATCH, DEC_SEQ, D), f32),
        'cache_k': nrm(ks[2], kv_shape, f32),
        'cache_v': nrm(ks[3], kv_shape, f32),
        'c': nrm(ks[4], (DEC_BATCH, D), f32),
        'c_ctx': nrm(ks[5], (D,), f32),
        'w_mod': nrm(ks[6], (DEPTH, D, 6 * D), f32) * (0.5 * D ** -0.5),
        'b_mod': nrm(ks[7], (DEPTH, 6 * D), f32) * 0.02,
        'g_norm': 1.0 + 0.05 * nrm(ks[8], (DEPTH, 4, D), f32),
        'w_in': nrm(ks[9], (DEPTH, D, IN_COLS), f32) * D ** -0.5,
        'lam_p': nrm(ks[10], (DEPTH, 4, HEAD_DIM_A), f32) * 0.1,
        'g_subln': 1.0 + 0.05 * nrm(ks[11], (DEPTH, 2 * HEAD_DIM_A), f32),
        'conv_w': nrm(ks[12], (DEPTH, 3, W_B), f32) * 3 ** -0.5,
        'pool_w': nrm(ks[13], (DEPTH, len(POOL_WINDOWS), POOL_GROUP, POOL_GROUP), f32) * POOL_GROUP ** -0.5,
        'pool_scale': 1.0 + 0.1 * nrm(ks[14], (DEPTH, W_C), f32),
        'w_branch': nrm(ks[15], (DEPTH, N_BRANCH, W_A, D), f32) * W_A ** -0.5,
        'w_gate': nrm(ks[16], (DEPTH, D, N_BRANCH * D), f32) * D ** -0.5,
        'b_gate': nrm(ks[17], (DEPTH, N_BRANCH * D), f32) * 0.02,
        'w_out': nrm(ks[18], (DEPTH, D, D), f32) * D ** -0.5,
        'w_router': nrm(ks[19], (DEPTH, D, N_EXPERTS), f32) * D ** -0.5,
        'w_e1': nrm(ks[20], (DEPTH, N_EXPERTS, D, D_FF_EXPERT), f32) * D ** -0.5,
        'w_e3': nrm(ks[21], (DEPTH, N_EXPERTS, D, D_FF_EXPERT), f32) * D ** -0.5,
        'w_e2': nrm(ks[22], (DEPTH, N_EXPERTS, D_FF_EXPERT, D), f32) * D_FF_EXPERT ** -0.5,
    }


def reference(x_prompt, x_sample, cache_k, cache_v, c, c_ctx, w_mod, b_mod, g_norm, w_in, lam_p, g_subln,
              conv_w, pool_w, pool_scale, w_branch, w_gate, b_gate, w_out, w_router, w_e1, w_e3, w_e2):
    layers = [dict(w_mod=w_mod[l], b_mod=b_mod[l], g_norm=g_norm[l], w_in=w_in[l], lam_p=lam_p[l],
                   g_subln=g_subln[l], conv_w=conv_w[l], pool_w=pool_w[l], pool_scale=pool_scale[l],
                   w_branch=w_branch[l], w_gate=w_gate[l], b_gate=b_gate[l], w_out=w_out[l],
                   w_router=w_router[l], w_e1=w_e1[l], w_e3=w_e3[l], w_e2=w_e2[l]) for l in range(DEPTH)]
    lam_inits = [0.8 - 0.6 * math.exp(-0.3 * l) for l in range(DEPTH)]

    xp = x_prompt
    ks_, vs_ = [], []
    for l in range(DEPTH):
        xp, k_l, v_l = trunk_layer(xp, c_ctx[None, :], layers[l], lam_inits[l], None)
        ks_.append(k_l)
        vs_.append(v_l)
    new_k = jnp.stack(ks_, axis=1)
    new_v = jnp.stack(vs_, axis=1)

    xs = x_sample
    for l in range(DEPTH):
        xs, _, _ = trunk_layer(xs, c, layers[l], lam_inits[l], (cache_k[:, l], cache_v[:, l]))

    return (xp, xs, new_k, new_v)
```

```python
import functools
import math

import jax
import jax.numpy as jnp
from jax import lax
from jax.experimental import pallas as pl
from jax.experimental.pallas import tpu as pltpu

D = 2048
BATCH = 16
SEQ = 256
DEPTH = 2
DEC_BATCH = 4
DEC_SEQ = 4096
PAST_LEN = 256
GRID_W = 64
HD = 64
HEAD_W = 2 * HD
W_A = D // 2
N_HEADS = W_A // HEAD_W
W_B = D // 2
W_C = D // 2
POOL_WINDOWS = (2, 4, 8, 16)
POOL_GROUP = W_C // len(POOL_WINDOWS)
N_EXPERTS = 16
CAPACITY_FACTOR = 2
ROPE_BASE = 10000.0
EPS = 1e-6

R_CTX = BATCH * SEQ
R_LAT = DEC_BATCH * DEC_SEQ
R = R_CTX + R_LAT
N_MODROWS = 8
CAP_CTX = CAPACITY_FACTOR * SEQ // N_EXPERTS
CAP_LAT = CAPACITY_FACTOR * DEC_SEQ // N_EXPERTS
ROWS_PER_EXPERT = BATCH * CAP_CTX + DEC_BATCH * CAP_LAT

VMEM_LIMIT = 52 * 1024 * 1024

bf16 = jnp.bfloat16
f32 = jnp.float32


def _cparams(sem):
    return pltpu.CompilerParams(dimension_semantics=sem, vmem_limit_bytes=VMEM_LIMIT)


def _mod_row(i, tm):
    r0 = i * tm
    return jnp.where(r0 < R_CTX, 0, 1 + (r0 - R_CTX) // DEC_SEQ)


def _mod_kernel(c_ref, w_ref, b_ref, o_ref):
    c = c_ref[...]
    s = (c * jax.nn.sigmoid(c)).astype(bf16)
    o_ref[...] = jnp.dot(s, w_ref[...].astype(bf16), preferred_element_type=f32) + b_ref[...]


def _mod_vectors(c8, w_mod, b_mod):
    tn = 1024
    return pl.pallas_call(
        _mod_kernel,
        grid=(DEPTH, 6 * D // tn),
        in_specs=[pl.BlockSpec((N_MODROWS, D), lambda l, j: (0, 0)),
                  pl.BlockSpec((None, D, tn), lambda l, j: (l, 0, j)),
                  pl.BlockSpec((None, 1, tn), lambda l, j: (l, 0, j))],
        out_specs=pl.BlockSpec((None, N_MODROWS, tn), lambda l, j: (l, 0, j)),
        out_shape=jax.ShapeDtypeStruct((DEPTH, N_MODROWS, 6 * D), f32),
        compiler_params=_cparams(("parallel", "parallel")),
        name="mod_vectors",
    )(c8, w_mod, b_mod.reshape(DEPTH, 1, 6 * D))


def _rms(x, g):
    ms = jnp.mean(x * x, axis=-1, keepdims=True)
    return x * lax.rsqrt(ms + EPS) * g


def _prenorm_kernel(x_ref, g_ref, sh_ref, sc_ref, h_ref):
    h = _rms(x_ref[...], g_ref[...]) * (1.0 + sc_ref[...]) + sh_ref[...]
    h_ref[...] = h.astype(h_ref.dtype)


def _resid_prenorm_kernel(x_ref, y_ref, gy_ref, gt_ref, g_ref, sh_ref, sc_ref, xo_ref, h_ref):
    x = x_ref[...] + gt_ref[...] * _rms(y_ref[...], gy_ref[...])
    xo_ref[...] = x
    h = _rms(x, g_ref[...]) * (1.0 + sc_ref[...]) + sh_ref[...]
    h_ref[...] = h.astype(h_ref.dtype)


def _resid_kernel(x_ref, y_ref, gy_ref, gt_ref, xo_ref):
    xo_ref[...] = x_ref[...] + gt_ref[...] * _rms(y_ref[...], gy_ref[...])


def _row_spec(tm):
    return pl.BlockSpec((tm, D), lambda i: (i, 0))


def _vec_spec(row):
    return pl.BlockSpec((None, 1, D), lambda i: (row, 0, 0))


def _mod_spec(chunk, tm):
    return pl.BlockSpec((None, 1, D), lambda i: (_mod_row(i, tm), 0, chunk))


def _prenorm(x, gn, modl):
    tm = 512
    return pl.pallas_call(
        _prenorm_kernel, grid=(R // tm,),
        in_specs=[_row_spec(tm), _vec_spec(0), _mod_spec(0, tm), _mod_spec(1, tm)],
        out_specs=_row_spec(tm),
        out_shape=jax.ShapeDtypeStruct((R, D), bf16),
        compiler_params=_cparams(("parallel",)), name="prenorm",
    )(x, gn, modl, modl)


def _resid_prenorm(x, y, gn_prev, modl_prev, gn, modl):
    tm = 512
    return pl.pallas_call(
        _resid_prenorm_kernel, grid=(R // tm,),
        in_specs=[_row_spec(tm), _row_spec(tm), _vec_spec(3), _mod_spec(5, tm),
                  _vec_spec(0), _mod_spec(0, tm), _mod_spec(1, tm)],
        out_specs=[_row_spec(tm), _row_spec(tm)],
        out_shape=[jax.ShapeDtypeStruct((R, D), f32), jax.ShapeDtypeStruct((R, D), bf16)],
        compiler_params=_cparams(("parallel",)), name="resid_prenorm",
    )(x, y, gn_prev, modl_prev, gn, modl, modl)


def _resid(x, y, gn_prev, modl_prev):
    tm = 512
    return pl.pallas_call(
        _resid_kernel, grid=(R // tm,),
        in_specs=[_row_spec(tm), _row_spec(tm), _vec_spec(3), _mod_spec(5, tm)],
        out_specs=_row_spec(tm),
        out_shape=jax.ShapeDtypeStruct((R, D), f32),
        compiler_params=_cparams(("parallel",)), name="resid",
    )(x, y, gn_prev, modl_prev)


def _mm_kernel(a_ref, b_ref, *rest, sigmoid_bias):
    acc = jnp.dot(a_ref[...], b_ref[...], preferred_element_type=f32)
    if sigmoid_bias:
        bias_ref, o_ref = rest
        acc = jax.nn.sigmoid(acc + bias_ref[...])
    else:
        (o_ref,) = rest
    o_ref[...] = acc.astype(o_ref.dtype)


def _mm(a, b, out_dtype, bias=None, tm=512, tn=1024, name="mm"):
    m, k = a.shape
    n = b.shape[1]
    in_specs = [pl.BlockSpec((tm, k), lambda j, i: (i, 0)),
                pl.BlockSpec((k, tn), lambda j, i: (0, j))]
    args = [a, b]
    if bias is not None:
        in_specs.append(pl.BlockSpec((1, tn), lambda j, i: (0, j)))
        args.append(bias.reshape(1, n))
    return pl.pallas_call(
        functools.partial(_mm_kernel, sigmoid_bias=bias is not None),
        grid=(n // tn, m // tm),
        in_specs=in_specs,
        out_specs=pl.BlockSpec((tm, tn), lambda j, i: (i, j)),
        out_shape=jax.ShapeDtypeStruct((m, n), out_dtype),
        compiler_params=_cparams(("parallel", "parallel")), name=name,
    )(*args)


_NT = (((1,), (1,)), ((), ()))


def _rope_apply(x, cos, sin_signed):
    lane = lax.broadcasted_iota(jnp.int32, x.shape, 1)
    partner = jnp.where((lane % 32) < 16, pltpu.roll(x, HEAD_W - 16, 1), pltpu.roll(x, 16, 1))
    return x * cos + partner * sin_signed


def _attn_kernel(lamp_ref, gs_ref, q_ref, k_ref, v_ref, *rest, lam_init, n_own, n_cache, tq, tk, rope):
    if rope:
        ck_ref, cv_ref, cosk_ref, sink_ref, cosq_ref, sinq_ref, o_ref, k_scr, v_scr = rest
    else:
        o_ref, k_scr, v_scr = rest
    n_keys = n_own + n_cache

    @pl.when(pl.program_id(2) == 0)
    def _():
        k = k_ref[...]
        if rope:
            k = _rope_apply(k, cosk_ref[...], sink_ref[...])
        k_scr[0:n_own, :] = k.astype(bf16)
        v_scr[0:n_own, :] = v_ref[...].astype(bf16)
        if n_cache:
            k_scr[n_own:n_keys, :] = ck_ref[...].astype(bf16)
            v_scr[n_own:n_keys, :] = cv_ref[...].astype(bf16)

    q = q_ref[...]
    if rope:
        q = _rope_apply(q, cosq_ref[...], sinq_ref[...])
    q = q * (HD ** -0.5)
    lane = lax.broadcasted_iota(jnp.int32, q.shape, 1)
    q1 = jnp.where(lane < HD, q, 0.0).astype(bf16)
    q2 = jnp.where(lane >= HD, q, 0.0).astype(bf16)

    def update(s, vc, m, l, acc):
        m_new = jnp.maximum(m, jnp.max(s, axis=-1, keepdims=True))
        alpha = jnp.exp(m - m_new)
        p = jnp.exp(s - m_new)
        l = alpha * l + jnp.sum(p, axis=-1, keepdims=True)
        acc = alpha * acc + jnp.dot(p.astype(bf16), vc, preferred_element_type=f32)
        return m_new, l, acc

    def body(c, carry):
        m1, l1, a1, m2, l2, a2 = carry
        off = pl.multiple_of(c * tk, tk)
        kc = k_scr[pl.ds(off, tk), :]
        vc = v_scr[pl.ds(off, tk), :]
        s1 = lax.dot_general(q1, kc, _NT, preferred_element_type=f32)
        s2 = lax.dot_general(q2, kc, _NT, preferred_element_type=f32)
        m1, l1, a1 = update(s1, vc, m1, l1, a1)
        m2, l2, a2 = update(s2, vc, m2, l2, a2)
        return m1, l1, a1, m2, l2, a2

    neg = jnp.full((tq, 1), -jnp.inf, f32)
    zero1 = jnp.zeros((tq, 1), f32)
    zacc = jnp.zeros((tq, HEAD_W), f32)
    m1, l1, a1, m2, l2, a2 = lax.fori_loop(0, n_keys // tk, body, (neg, zero1, zacc, neg, zero1, zacc))

    lp = lamp_ref[...]
    lam = (jnp.exp(jnp.sum(lp[0:1, :] * lp[1:2, :], axis=-1, keepdims=True))
           - jnp.exp(jnp.sum(lp[2:3, :] * lp[3:4, :], axis=-1, keepdims=True)) + lam_init)
    o = a1 / l1 - lam * (a2 / l2)
    o = o * lax.rsqrt(jnp.mean(o * o, axis=-1, keepdims=True) + EPS) * gs_ref[...] * (1.0 - lam_init)
    o_ref[...] = o.astype(o_ref.dtype)


def _attention(qkv, lam_p, g_subln, lam_init, *, row_off, nb, n_own, tq, cache=None, rope_tabs=None):
    nq = n_own // tq
    n_cache = 0 if cache is None else PAST_LEN
    n_keys = n_own + n_cache
    tk = 256
    q_blk0 = row_off // tq
    kv_blk0 = row_off // n_own
    in_specs = [pl.BlockSpec((4, HD), lambda b, h, i: (0, 0)),
                pl.BlockSpec((1, HEAD_W), lambda b, h, i: (0, 0)),
                pl.BlockSpec((tq, HEAD_W), lambda b, h, i: (q_blk0 + b * nq + i, h)),
                pl.BlockSpec((n_own, HEAD_W), lambda b, h, i: (kv_blk0 + b, N_HEADS + h)),
                pl.BlockSpec((n_own, HEAD_W), lambda b, h, i: (kv_blk0 + b, 2 * N_HEADS + h))]
    args = [lam_p, g_subln.reshape(1, HEAD_W), qkv, qkv, qkv]
    if cache is not None:
        ck, cv = cache
        cos, sin_signed = rope_tabs
        in_specs += [pl.BlockSpec((None, PAST_LEN, HEAD_W), lambda b, h, i: (b, 0, h)),
                     pl.BlockSpec((None, PAST_LEN, HEAD_W), lambda b, h, i: (b, 0, h)),
                     pl.BlockSpec((n_own, HEAD_W), lambda b, h, i: (0, 0)),
                     pl.BlockSpec((n_own, HEAD_W), lambda b, h, i: (0, 0)),
                     pl.BlockSpec((tq, HEAD_W), lambda b, h, i: (i, 0)),
                     pl.BlockSpec((tq, HEAD_W), lambda b, h, i: (i, 0))]
        args += [ck, cv, cos, sin_signed, cos, sin_signed]
    kern = functools.partial(_attn_kernel, lam_init=lam_init, n_own=n_own, n_cache=n_cache, tq=tq, tk=tk,
                             rope=cache is not None)
    return pl.pallas_call(
        kern, grid=(nb, N_HEADS, nq),
        in_specs=in_specs,
        out_specs=pl.BlockSpec((tq, HEAD_W), lambda b, h, i: (b * nq + i, h)),
        out_shape=jax.ShapeDtypeStruct((nb * n_own, W_A), bf16),
        scratch_shapes=[pltpu.VMEM((n_keys, HEAD_W), bf16), pltpu.VMEM((n_keys, HEAD_W), bf16)],
        compiler_params=_cparams(("parallel", "parallel", "arbitrary")),
        name="diff_attention",
    )(*args)


def _rope_tables():
    n = DEC_SEQ
    pos = jnp.arange(n)
    row = (pos // GRID_W).astype(f32)
    col = (pos % GRID_W).astype(f32)
    n_freq = HD // 4
    inv = 1.0 / (ROPE_BASE ** (jnp.arange(n_freq, dtype=f32) / n_freq))
    lane = jnp.arange(HEAD_W)
    inv_l = inv[lane % n_freq]
    use_row = ((lane // 32) % 2) == 0
    ang = jnp.where(use_row[None, :], row[:, None], col[:, None]) * inv_l[None, :]
    sign = jnp.where((lane % 32) < 16, -1.0, 1.0).astype(f32)
    return jnp.cos(ang), jnp.sin(ang) * sign[None, :]


_PAD = 16


def _mix_kernel(cb_ref, cc_ref, ch_ref, pin_ref, cw_ref, pw_ref, ps_ref, yb_ref, yc_ref, u_scr, p_scr, *, n, rc):
    g = pl.program_id(1)
    zpad = jnp.zeros((_PAD, POOL_GROUP), f32)
    u_scr[0:_PAD, :] = zpad
    u_scr[_PAD + n:_PAD + n + _PAD, :] = zpad
    p_scr[0:_PAD, :] = zpad
    p_scr[_PAD + n:_PAD + n + _PAD, :] = zpad
    u_scr[_PAD:_PAD + n, :] = cc_ref[...].astype(f32) * ch_ref[...].astype(f32)
    p_scr[_PAD:_PAD + n, :] = pin_ref[...].astype(f32)
    cw = cw_ref[...]

    for c in range(n // rc):
        base = _PAD + c * rc
        conv = (u_scr[base - 1:base - 1 + rc, :] * cw[0:1, :] + u_scr[base:base + rc, :] * cw[1:2, :]
                + u_scr[base + 1:base + 1 + rc, :] * cw[2:3, :])
        yb_ref[c * rc:(c + 1) * rc, :] = (cb_ref[c * rc:(c + 1) * rc, :].astype(f32) * conv).astype(yb_ref.dtype)

    t = lax.broadcasted_iota(jnp.int32, (rc, 1), 0)
    for gi, win in enumerate(POOL_WINDOWS):
        @pl.when(g == gi)
        def _(win=win):
            half = win // 2
            for c in range(n // rc):
                base = _PAD + c * rc
                acc = p_scr[base - half:base - half + rc, :]
                for s in range(-half + 1, half):
                    acc = acc + p_scr[base + s:base + s + rc, :]
                tt = t + c * rc
                cnt = (jnp.minimum(tt + half, n) - jnp.maximum(tt - half, 0)).astype(f32)
                d = acc / cnt - p_scr[base:base + rc, :]
                y = jnp.dot(d.astype(bf16), pw_ref[...], preferred_element_type=f32) * ps_ref[...]
                yc_ref[c * rc:(c + 1) * rc, :] = y.astype(yc_ref.dtype)


def _mix(rest, conv_w, pool_w16, pool_scale, *, row_off, nb, n):
    ng = W_B // POOL_GROUP
    rc = 256
    blk0 = row_off // n
    in_specs = [pl.BlockSpec((n, POOL_GROUP), lambda b, g: (blk0 + b, g)),
                pl.BlockSpec((n, POOL_GROUP), lambda b, g: (blk0 + b, ng + g)),
                pl.BlockSpec((n, POOL_GROUP), lambda b, g: (blk0 + b, 2 * ng + g)),
                pl.BlockSpec((n, POOL_GROUP), lambda b, g: (blk0 + b, 3 * ng + g)),
                pl.BlockSpec((3, POOL_GROUP), lambda b, g: (0, g)),
                pl.BlockSpec((None, POOL_GROUP, POOL_GROUP), lambda b, g: (g, 0, 0)),
                pl.BlockSpec((1, POOL_GROUP), lambda b, g: (0, g))]
    out_spec = pl.BlockSpec((n, POOL_GROUP), lambda b, g: (b, g))
    return pl.pallas_call(
        functools.partial(_mix_kernel, n=n, rc=rc), grid=(nb, ng),
        in_specs=in_specs, out_specs=[out_spec, out_spec],
        out_shape=[jax.ShapeDtypeStruct((nb * n, W_B), bf16), jax.ShapeDtypeStruct((nb * n, W_C), bf16)],
        scratch_shapes=[pltpu.VMEM((n + 2 * _PAD, POOL_GROUP), f32), pltpu.VMEM((n + 2 * _PAD, POOL_GROUP), f32)],
        compiler_params=_cparams(("parallel", "parallel")), name="conv_pool_mix",
    )(rest, rest, rest, rest, conv_w, pool_w16, pool_scale.reshape(1, W_C))


def _merge_kernel(o_ref, yb_ref, yc_ref, wb_ref, ga_ref, gb_ref, gc_ref, out_ref):
    acc = ga_ref[...].astype(f32) * jnp.dot(o_ref[...], wb_ref[0], preferred_element_type=f32)
    acc += gb_ref[...].astype(f32) * jnp.dot(yb_ref[...], wb_ref[1], preferred_element_type=f32)
    acc += gc_ref[...].astype(f32) * jnp.dot(yc_ref[...], wb_ref[2], preferred_element_type=f32)
    out_ref[...] = acc.astype(out_ref.dtype)


def _merge(o, yb, yc, wb16, gates):
    tm, tn = 512, 1024
    nj = D // tn
    a_spec = pl.BlockSpec((tm, W_A), lambda j, i: (i, 0))
    return pl.pallas_call(
        _merge_kernel, grid=(nj, R // tm),
        in_specs=[a_spec, a_spec, a_spec,
                  pl.BlockSpec((3, W_A, tn), lambda j, i: (0, 0, j)),
                  pl.BlockSpec((tm, tn), lambda j, i: (i, j)),
                  pl.BlockSpec((tm, tn), lambda j, i: (i, nj + j)),
                  pl.BlockSpec((tm, tn), lambda j, i: (i, 2 * nj + j))],
        out_specs=pl.BlockSpec((tm, tn), lambda j, i: (i, j)),
        out_shape=jax.ShapeDtypeStruct((R, D), bf16),
        compiler_params=_cparams(("parallel", "parallel")), name="branch_merge",
    )(o, yb, yc, wb16, gates, gates, gates)


def _outproj_kernel(m_ref, w_ref, x_ref, g1_ref, gt_ref, g2_ref, sh_ref, sc_ref, wr_ref, xo_ref, h_ref, lg_ref):
    hmix = jnp.dot(m_ref[...], w_ref[...], preferred_element_type=f32)
    x = x_ref[...] + gt_ref[...] * _rms(hmix, g1_ref[...])
    xo_ref[...] = x
    h = _rms(x, g2_ref[...]) * (1.0 + sc_ref[...]) + sh_ref[...]
    h_ref[...] = h.astype(h_ref.dtype)
    lg_ref[...] = jnp.dot(h, wr_ref[...], preferred_element_type=f32, precision=lax.Precision.HIGHEST)


def _outproj(merged, w_out16, x, gn, modl, w_router):
    tm = 256
    return pl.pallas_call(
        _outproj_kernel, grid=(R // tm,),
        in_specs=[_row_spec(tm), pl.BlockSpec((D, D), lambda i: (0, 0)), _row_spec(tm),
                  _vec_spec(1), _mod_spec(2, tm), _vec_spec(2), _mod_spec(3, tm), _mod_spec(4, tm),
                  pl.BlockSpec((D, N_EXPERTS), lambda i: (0, 0))],
        out_specs=[_row_spec(tm), _row_spec(tm), pl.BlockSpec((tm, N_EXPERTS), lambda i: (i, 0))],
        out_shape=[jax.ShapeDtypeStruct((R, D), f32), jax.ShapeDtypeStruct((R, D), bf16),
                   jax.ShapeDtypeStruct((R, N_EXPERTS), f32)],
        compiler_params=_cparams(("parallel",)), name="outproj_resid_router",
    )(merged, w_out16, x, gn, modl, gn, modl, modl, w_router)


def _expert_kernel(x_ref, w1_ref, w3_ref, w2_ref, g_ref, o_ref, acc_ref, *, nf):
    f = pl.program_id(2)
    x = x_ref[...]
    a = jnp.dot(x, w1_ref[...], preferred_element_type=f32)
    u = jnp.dot(x, w3_ref[...], preferred_element_type=f32)
    mid = (a * jax.nn.sigmoid(a) * u).astype(bf16)
    y = jnp.dot(mid, w2_ref[...], preferred_element_type=f32)

    @pl.when(f == 0)
    def _():
        acc_ref[...] = y

    @pl.when(f > 0)
    def _():
        acc_ref[...] += y

    @pl.when(f == nf - 1)
    def _():
        o_ref[...] = (acc_ref[...] * g_ref[...]).astype(o_ref.dtype)


def _experts(xs, w1, w3, w2, gate):
    tm, tf = 640, 512
    nf = D // tf
    return pl.pallas_call(
        functools.partial(_expert_kernel, nf=nf), grid=(N_EXPERTS, ROWS_PER_EXPERT // tm, nf),
        in_specs=[pl.BlockSpec((None, tm, D), lambda e, m, f: (e, m, 0)),
                  pl.BlockSpec((None, D, tf), lambda e, m, f: (e, 0, f)),
                  pl.BlockSpec((None, D, tf), lambda e, m, f: (e, 0, f)),
                  pl.BlockSpec((None, tf, D), lambda e, m, f: (e, f, 0)),
                  pl.BlockSpec((None, tm, 1), lambda e, m, f: (e, m, 0))],
        out_specs=pl.BlockSpec((None, tm, D), lambda e, m, f: (e, m, 0)),
        out_shape=jax.ShapeDtypeStruct((N_EXPERTS, ROWS_PER_EXPERT, D), f32),
        scratch_shapes=[pltpu.VMEM((tm, D), f32)],
        compiler_params=_cparams(("parallel", "parallel", "arbitrary")), name="expert_swiglu",
    )(xs, w1, w3, w2, gate)


def _route_stream(h2, logits, nb, n):
    cap = CAPACITY_FACTOR * n // N_EXPERTS
    aff = jax.nn.softmax(logits.reshape(nb, n, N_EXPERTS), axis=-1)
    gate, idx = lax.top_k(jnp.swapaxes(aff, 1, 2), cap)
    xs = jax.vmap(lambda hr, i: hr[i])(h2.reshape(nb, n, D), idx)
    xs = jnp.swapaxes(xs, 0, 1).reshape(N_EXPERTS, nb * cap, D)
    gate = jnp.swapaxes(gate, 0, 1).reshape(N_EXPERTS, nb * cap, 1)
    return xs, gate, idx


def _combine_stream(y, idx, nb, n):
    cap = idx.shape[-1]
    y = jnp.swapaxes(y.reshape(N_EXPERTS, nb, cap, D), 0, 1).reshape(nb, N_EXPERTS * cap, D)
    out = jax.vmap(lambda yr, i: jnp.zeros((n, D), f32).at[i.reshape(-1)].add(yr))(y, idx)
    return out.reshape(nb * n, D)


def kernel(x_prompt, x_sample, cache_k, cache_v, c, c_ctx, w_mod, b_mod, g_norm, w_in, lam_p, g_subln, conv_w,
           pool_w, pool_scale, w_branch, w_gate, b_gate, w_out, w_router, w_e1, w_e3, w_e2):
    x = jnp.concatenate([x_prompt.reshape(R_CTX, D), x_sample.reshape(R_LAT, D)], axis=0)
    c8 = jnp.concatenate([c_ctx[None, :], c, jnp.zeros((N_MODROWS - 1 - DEC_BATCH, D), f32)], axis=0)
    mod = _mod_vectors(c8, w_mod, b_mod).reshape(DEPTH, N_MODROWS, 1, 6 * D)
    rope_tabs = _rope_tables()
    ck = cache_k.reshape(DEC_BATCH, DEPTH, PAST_LEN, W_A)
    cv = cache_v.reshape(DEC_BATCH, DEPTH, PAST_LEN, W_A)
    n_qkv = 3 * W_A

    new_k, new_v = [], []
    hff = None
    for l in range(DEPTH):
        lam_init = 0.8 - 0.6 * math.exp(-0.3 * l)
        modl = mod[l]
        gn = g_norm[l].reshape(4, 1, D)
        if l == 0:
            h = _prenorm(x, gn, modl)
        else:
            x, h = _resid_prenorm(x, hff, g_norm[l - 1].reshape(4, 1, D), mod[l - 1], gn, modl)

        w_in16 = w_in[l].astype(bf16)
        qkv = _mm(h, w_in16[:, :n_qkv], f32, name="in_proj_qkv")
        rest = _mm(h, w_in16[:, n_qkv:], bf16, name="in_proj_mix")
        gates = _mm(h, w_gate[l].astype(bf16), bf16, bias=b_gate[l], name="gate_proj")
        new_k.append(qkv[:R_CTX, W_A:2 * W_A].reshape(BATCH, SEQ, N_HEADS, HEAD_W))
        new_v.append(qkv[:R_CTX, 2 * W_A:].reshape(BATCH, SEQ, N_HEADS, HEAD_W))

        o_ctx = _attention(qkv, lam_p[l], g_subln[l], lam_init, row_off=0, nb=BATCH, n_own=SEQ, tq=SEQ)
        o_lat = _attention(qkv, lam_p[l], g_subln[l], lam_init, row_off=R_CTX, nb=DEC_BATCH, n_own=DEC_SEQ, tq=256,
                           cache=(ck[:, l], cv[:, l]), rope_tabs=rope_tabs)
        pool_w16 = pool_w[l].astype(bf16)
        yb_ctx, yc_ctx = _mix(rest, conv_w[l], pool_w16, pool_scale[l], row_off=0, nb=BATCH, n=SEQ)
        yb_lat, yc_lat = _mix(rest, conv_w[l], pool_w16, pool_scale[l], row_off=R_CTX, nb=DEC_BATCH, n=DEC_SEQ)
        o = jnp.concatenate([o_ctx, o_lat], axis=0)
        yb = jnp.concatenate([yb_ctx, yb_lat], axis=0)
        yc = jnp.concatenate([yc_ctx, yc_lat], axis=0)

        merged = _merge(o, yb, yc, w_branch[l].astype(bf16), gates)
        x, h2, logits = _outproj(merged, w_out[l].astype(bf16), x, gn, modl, w_router[l])

        xs_c, gate_c, idx_c = _route_stream(h2[:R_CTX], logits[:R_CTX], BATCH, SEQ)
        xs_l, gate_l, idx_l = _route_stream(h2[R_CTX:], logits[R_CTX:], DEC_BATCH, DEC_SEQ)
        xs = jnp.concatenate([xs_c, xs_l], axis=1)
        gate = jnp.concatenate([gate_c, gate_l], axis=1)
        y = _experts(xs, w_e1[l].astype(bf16), w_e3[l].astype(bf16), w_e2[l].astype(bf16), gate)
        n_c = BATCH * CAP_CTX
        hff = jnp.concatenate([_combine_stream(y[:, :n_c], idx_c, BATCH, SEQ),
                               _combine_stream(y[:, n_c:], idx_l, DEC_BATCH, DEC_SEQ)], axis=0)

    x = _resid(x, hff, g_norm[DEPTH - 1].reshape(4, 1, D), mod[DEPTH - 1])
    y_prompt = x[:R_CTX].reshape(BATCH, SEQ, D)
    y_sample = x[R_CTX:].reshape(DEC_BATCH, DEC_SEQ, D)
    return y_prompt, y_sample, jnp.stack(new_k, axis=1), jnp.stack(new_v, axis=1)
```
